```python
import math
import jax, jax.numpy as jnp
from jax import lax
import numpy as np

D_MODEL = 1024
BATCH = 1
SEQ = 16384
DEPTH = 1
DEC_BATCH = 16
DEC_SEQ = 16
PAST_LEN = 2048

CHUNK = 64
Q_BLOCK = 128
HEAD_DIM = 64
N_DIFF_HEADS = 4
N_FOX_HEADS = 8
A_WIDTH = N_DIFF_HEADS * 2 * HEAD_DIM
B_WIDTH = N_FOX_HEADS * HEAD_DIM
D_FF = ((8 * D_MODEL // 3 + 255) // 256) * 256
NUM_BUCKETS = 32
MAX_DISTANCE = 128
ALPHA = (2 * DEPTH) ** 0.25
BETA = (8 * DEPTH) ** -0.25
FORGET_BIAS = 3.0
LN_EPS = 1e-5
NEG = -1e30
IN_COLS = 3 * A_WIDTH + 3 * B_WIDTH + N_FOX_HEADS + 2 * D_MODEL
SPLITS = [int(v) for v in np.cumsum([A_WIDTH] * 3 + [B_WIDTH] * 3 + [N_FOX_HEADS])]

kernel_name = 'diff_fox_hybrid_stream_step'


def layer_norm(x, g, b):
    x32 = x.astype(jnp.float32)
    mu = jnp.mean(x32, axis=-1, keepdims=True)
    var = jnp.mean(jnp.square(x32 - mu), axis=-1, keepdims=True)
    return ((x32 - mu) * lax.rsqrt(var + LN_EPS) * g + b).astype(x.dtype)


def modulation(c, w, b):
    m = jax.nn.silu(c) @ w + b
    return [t[:, None, :] for t in jnp.split(m, 6, axis=-1)]


def t5_bucket(rel):
    nb = NUM_BUCKETS // 2
    ret = jnp.where(rel > 0, nb, 0)
    n = jnp.abs(rel)
    max_exact = nb // 2
    nf = jnp.maximum(n, 1).astype(jnp.float32)
    large = max_exact + (jnp.log(nf / max_exact) / math.log(MAX_DISTANCE / max_exact)
                         * (nb - max_exact)).astype(jnp.int32)
    large = jnp.minimum(large, nb - 1)
    return ret + jnp.where(n < max_exact, n, large)


def mixer_inputs(h, w_in_l, b_forget_l):
    B, T, _ = h.shape
    z = h @ w_in_l
    qa, ka, va, qb, kb, vb, f_logit, gates = jnp.split(z, SPLITS, axis=-1)
    heads_a = lambda a: a.reshape(B, T, N_DIFF_HEADS, 2 * HEAD_DIM)
    heads_b = lambda a: a.reshape(B, T, N_FOX_HEADS, HEAD_DIM)
    logf = jax.nn.log_sigmoid((f_logit + b_forget_l).astype(jnp.float32))
    ga, gb = jnp.split(gates, 2, axis=-1)
    return (heads_a(qa), heads_a(ka), heads_a(va), heads_b(qb), heads_b(kb), heads_b(vb), logf, ga, gb)


def diff_attention(qa, ka, va, q_pos, k_pos, rel_bias, lam):
    mask = (k_pos[None, :] // CHUNK) <= (q_pos[:, None] // CHUNK)
    bias = jnp.transpose(rel_bias[t5_bucket(k_pos[None, :] - q_pos[:, None])], (2, 0, 1)).astype(jnp.float32)
    scale = HEAD_DIM ** -0.5

    def probs(q, k):
        s = jnp.einsum('bqhd,bkhd->bhqk', q, k).astype(jnp.float32) * scale + bias
        return jax.nn.softmax(jnp.where(mask, s, NEG), axis=-1)

    p = probs(qa[..., :HEAD_DIM], ka[..., :HEAD_DIM]) - lam * probs(qa[..., HEAD_DIM:], ka[..., HEAD_DIM:])
    return jnp.einsum('bhqk,bkhe->bqhe', p.astype(va.dtype), va)


def fox_attention(q, k, v, fq, fk, q_pos, k_pos):
    mask = k_pos[None, :] <= q_pos[:, None]
    decay = jnp.transpose(fq, (0, 2, 1))[..., :, None] - jnp.transpose(fk, (0, 2, 1))[..., None, :]
    s = jnp.einsum('bqhd,bkhd->bhqk', q, k).astype(jnp.float32) * (HEAD_DIM ** -0.5) + decay
    p = jax.nn.softmax(jnp.where(mask, s, NEG), axis=-1)
    return jnp.einsum('bhqk,bkhd->bqhd', p.astype(v.dtype), v)


def prompt_attention(qa, ka, va, qb, kb, vb, F, rel_bias, lam):
    B, S = qa.shape[:2]
    k_pos = jnp.arange(S, dtype=jnp.int32)

    def block(i):
        start = i * Q_BLOCK
        sl = lambda a: lax.dynamic_slice_in_dim(a, start, Q_BLOCK, axis=1)
        q_pos = start + jnp.arange(Q_BLOCK, dtype=jnp.int32)
        oa = diff_attention(sl(qa), ka, va, q_pos, k_pos, rel_bias, lam)
        ob = fox_attention(sl(qb), kb, vb, sl(F), F, q_pos, k_pos)
        return oa, ob

    oa, ob = lax.map(block, jnp.arange(S // Q_BLOCK, dtype=jnp.int32))
    unblock = lambda o: jnp.moveaxis(o, 0, 1).reshape(B, S, o.shape[3], o.shape[4])
    return unblock(oa), unblock(ob)


def mixer_output(oa, ob, ga, gb, subln_g_l, w_a, w_b, w_o_l, lam_init):
    B, T = oa.shape[:2]
    oa32 = oa.astype(jnp.float32)
    oa = (oa32 * lax.rsqrt(jnp.mean(jnp.square(oa32), axis=-1, keepdims=True) + LN_EPS)
          * subln_g_l * (1.0 - lam_init)).astype(ob.dtype)
    pa = oa.reshape(B, T, A_WIDTH) @ w_a
    pb = ob.reshape(B, T, B_WIDTH) @ w_b
    return (jax.nn.sigmoid(ga) * pa + jax.nn.sigmoid(gb) * pb) @ w_o_l


def post_mixer(x, y_mix, mod, ln1_g_l, ln1_b_l, ln2_g_l, ln2_b_l, w_ffn_in_l, w_ffn_out_l):
    _, _, gate1, shift2, scale2, gate2 = mod
    x1 = layer_norm(ALPHA * x + gate1 * y_mix, ln1_g_l, ln1_b_l)
    h2 = x1 * (1 + scale2) + shift2
    g, u = jnp.split(h2 @ w_ffn_in_l, 2, axis=-1)
    f = (jax.nn.silu(g) * u) @ w_ffn_out_l
    return layer_norm(ALPHA * x1 + gate2 * f, ln2_g_l, ln2_b_l)


def setup_inputs(seed: int = 0) -> dict:
    key = jax.random.key(seed)
    ks = jax.random.split(key, 32)
    nrm = lambda k, shape, s=1.0: jax.random.normal(k, shape, jnp.float32) * s
    col_scale = jnp.concatenate([
        jnp.ones((2 * A_WIDTH,), jnp.float32), jnp.full((A_WIDTH,), BETA, jnp.float32),
        jnp.ones((2 * B_WIDTH,), jnp.float32), jnp.full((B_WIDTH,), BETA, jnp.float32),
        jnp.ones((N_FOX_HEADS + 2 * D_MODEL,), jnp.float32)])
    return {
        'x_prompt': nrm(ks[0], (BATCH, SEQ, D_MODEL)),
        'x_sample': nrm(ks[1], (DEC_BATCH, DEC_SEQ, D_MODEL)),
        'cache_diff_k': nrm(ks[2], (DEPTH, DEC_BATCH, PAST_LEN, N_DIFF_HEADS, 2 * HEAD_DIM)),
        'cache_diff_v': nrm(ks[3], (DEPTH, DEC_BATCH, PAST_LEN, N_DIFF_HEADS, 2 * HEAD_DIM), BETA),
        'cache_fox_k': nrm(ks[4], (DEPTH, DEC_BATCH, PAST_LEN, N_FOX_HEADS, HEAD_DIM)),
        'cache_fox_v': nrm(ks[5], (DEPTH, DEC_BATCH, PAST_LEN, N_FOX_HEADS, HEAD_DIM), BETA),
        'cache_fox_logf': jax.nn.log_sigmoid(FORGET_BIAS + nrm(ks[6], (DEPTH, DEC_BATCH, PAST_LEN, N_FOX_HEADS))),
        'c_prompt': nrm(ks[7], (BATCH, D_MODEL)),
        'c_sample': nrm(ks[8], (DEC_BATCH, D_MODEL)),
        'w_ada': nrm(ks[9], (DEPTH, D_MODEL, 6 * D_MODEL), D_MODEL ** -0.5),
        'b_ada': nrm(ks[10], (DEPTH, 6 * D_MODEL), 0.02),
        'w_in': nrm(ks[11], (DEPTH, D_MODEL, IN_COLS), D_MODEL ** -0.5) * col_scale,
        'b_forget': FORGET_BIAS + nrm(ks[12], (DEPTH, N_FOX_HEADS), 0.1),
        'lambda_q1': nrm(ks[13], (DEPTH, HEAD_DIM), 0.1),
        'lambda_k1': nrm(ks[14], (DEPTH, HEAD_DIM), 0.1),
        'lambda_q2': nrm(ks[15], (DEPTH, HEAD_DIM), 0.1),
        'lambda_k2': nrm(ks[16], (DEPTH, HEAD_DIM), 0.1),
        'subln_g': 1.0 + nrm(ks[17], (DEPTH, 2 * HEAD_DIM), 0.02),
        'rel_bias': nrm(ks[18], (NUM_BUCKETS, N_DIFF_HEADS), 0.5),
        'w_branch_a': nrm(ks[19], (DEPTH, A_WIDTH, D_MODEL), A_WIDTH ** -0.5 * BETA),
        'w_branch_b': nrm(ks[20], (DEPTH, B_WIDTH, D_MODEL), B_WIDTH ** -0.5 * BETA),
        'w_o': nrm(ks[21], (DEPTH, D_MODEL, D_MODEL), D_MODEL ** -0.5 * BETA),
        'ln1_g': 1.0 + nrm(ks[22], (DEPTH, D_MODEL), 0.02),
        'ln1_b': nrm(ks[23], (DEPTH, D_MODEL), 0.02),
        'ln2_g': 1.0 + nrm(ks[24], (DEPTH, D_MODEL), 0.02),
        'ln2_b': nrm(ks[25], (DEPTH, D_MODEL), 0.02),
        'w_ffn_in': nrm(ks[26], (DEPTH, D_MODEL, 2 * D_FF), D_MODEL ** -0.5 * BETA),
        'w_ffn_out': nrm(ks[27], (DEPTH, D_FF, D_MODEL), D_FF ** -0.5 * BETA),
    }


def reference(x_prompt, x_sample, cache_diff_k, cache_diff_v, cache_fox_k, cache_fox_v, cache_fox_logf,
              c_prompt, c_sample, w_ada, b_ada, w_in, b_forget, lambda_q1, lambda_k1, lambda_q2, lambda_k2,
              subln_g, rel_bias, w_branch_a, w_branch_b, w_o, ln1_g, ln1_b, ln2_g, ln2_b, w_ffn_in, w_ffn_out):
    f32 = jnp.float32
    xp, xs = x_prompt, x_sample
    past = cache_diff_k.shape[2]
    dk_p, dv_p, fk_p, fv_p, fl_p = [], [], [], [], []
    dk_s, dv_s, fk_s, fv_s, fl_s = [], [], [], [], []
    for l in range(DEPTH):
        lam_init = 0.8 - 0.6 * math.exp(-0.3 * l)
        lam = (jnp.exp(jnp.sum(lambda_q1[l].astype(f32) * lambda_k1[l].astype(f32)))
               - jnp.exp(jnp.sum(lambda_q2[l].astype(f32) * lambda_k2[l].astype(f32))) + lam_init)
        ffn_w = (ln1_g[l], ln1_b[l], ln2_g[l], ln2_b[l], w_ffn_in[l], w_ffn_out[l])

        mod_p = modulation(c_prompt, w_ada[l], b_ada[l])
        hp = xp * (1 + mod_p[1]) + mod_p[0]
        qa, ka, va, qb, kb, vb, logf, ga, gb = mixer_inputs(hp, w_in[l], b_forget[l])
        F = jnp.cumsum(logf, axis=1)
        oa, ob = prompt_attention(qa, ka, va, qb, kb, vb, F, rel_bias, lam)
        y_mix = mixer_output(oa, ob, ga, gb, subln_g[l], w_branch_a[l], w_branch_b[l], w_o[l], lam_init)
        xp_next = post_mixer(xp, y_mix, mod_p, *ffn_w)
        dk_p.append(ka); dv_p.append(va); fk_p.append(kb); fv_p.append(vb); fl_p.append(logf.astype(xp.dtype))

        mod_s = modulation(c_sample, w_ada[l], b_ada[l])
        hs = xs * (1 + mod_s[1]) + mod_s[0]
        qa_s, ka_s, va_s, qb_s, kb_s, vb_s, logf_s, ga_s, gb_s = mixer_inputs(hs, w_in[l], b_forget[l])
        T = xs.shape[1]
        k_pos = jnp.arange(past + T, dtype=jnp.int32)
        q_pos = past + jnp.arange(T, dtype=jnp.int32)
        ka_all = jnp.concatenate([cache_diff_k[l], ka_s], axis=1)
        va_all = jnp.concatenate([cache_diff_v[l], va_s], axis=1)
        kb_all = jnp.concatenate([cache_fox_k[l], kb_s], axis=1)
        vb_all = jnp.concatenate([cache_fox_v[l], vb_s], axis=1)
        F_all = jnp.cumsum(jnp.concatenate([cache_fox_logf[l].astype(f32), logf_s], axis=1), axis=1)
        oa_s = diff_attention(qa_s, ka_all, va_all, q_pos, k_pos, rel_bias, lam)
        ob_s = fox_attention(qb_s, kb_all, vb_all, F_all[:, past:], F_all, q_pos, k_pos)
        y_mix_s = mixer_output(oa_s, ob_s, ga_s, gb_s, subln_g[l], w_branch_a[l], w_branch_b[l], w_o[l], lam_init)
        xs_next = post_mixer(xs, y_mix_s, mod_s, *ffn_w)
        dk_s.append(ka_s); dv_s.append(va_s); fk_s.append(kb_s); fv_s.append(vb_s); fl_s.append(logf_s.astype(xs.dtype))

        xp, xs = xp_next, xs_next

    st = lambda lst: jnp.stack(lst, axis=0)
    return (xp, xs, st(dk_p), st(dv_p), st(fk_p), st(fv_p), st(fl_p),
            st(dk_s), st(dv_s), st(fk_s), st(fv_s), st(fl_s))
```

```python
import functools
import math

import jax
import jax.numpy as jnp
from jax import lax
from jax.experimental import pallas as pl
from jax.experimental.pallas import tpu as pltpu

F32 = jnp.float32
BF16 = jnp.bfloat16

D_MODEL = 1024
HEAD_DIM = 64
N_DIFF_HEADS = 4
N_FOX_HEADS = 8
A_WIDTH = N_DIFF_HEADS * 2 * HEAD_DIM
B_WIDTH = N_FOX_HEADS * HEAD_DIM
QKV_COLS = 3 * A_WIDTH + 3 * B_WIDTH
D_FF = 2816
CHUNK_SHIFT = 6
NUM_BUCKETS = 32
ALPHA = 2.0 ** 0.25
LAM_INIT = 0.8 - 0.6 * math.exp(0.0)
LN_EPS = 1e-5
NEG = -1e30
SCALE = HEAD_DIM ** -0.5
LANES = 128
FFN_COL = 256
FAR_BUCKET = 15

_NT = (((1,), (1,)), ((), ()))


def _dot(a, b):
    return jnp.dot(a, b, preferred_element_type=F32)


def _dot_nt(a, b):
    return lax.dot_general(a, b, _NT, preferred_element_type=F32)


def _dot_f32(a, b):
    return jnp.dot(a, b, preferred_element_type=F32, precision=lax.Precision.HIGHEST)


def _sigmoid(x):
    return 1.0 / (1.0 + jnp.exp(-x))


def _layer_norm(r, g, b):
    mu = jnp.mean(r, axis=-1, keepdims=True)
    d = r - mu
    var = jnp.mean(d * d, axis=-1, keepdims=True)
    return d * lax.rsqrt(var + LN_EPS) * g + b


def _t5_bias(delta, rb_ref, h):
    n = jnp.abs(delta)
    n2 = n * n
    large = jnp.full(delta.shape, 8, jnp.int32)
    for j in range(1, 8):
        large = large + jnp.where(n2 >= (64 << j), 1, 0)
    bucket = jnp.where(delta > 0, NUM_BUCKETS // 2, 0) + jnp.where(n < 8, n, large)
    out = jnp.zeros(delta.shape, F32)
    for b in range(NUM_BUCKETS):
        out = jnp.where(bucket == b, rb_ref[b, h], out)
    return out


def _online_softmax_step(s, v, m, l, acc):
    m_new = jnp.maximum(m, jnp.max(s, axis=-1, keepdims=True))
    alpha = jnp.exp(m - m_new)
    p = jnp.exp(s - m_new)
    l = alpha * l + jnp.sum(p, axis=-1, keepdims=True)
    acc = alpha * acc + _dot(p.astype(BF16), v)
    return m_new, l, acc


def _split_halves(q):
    lane = lax.broadcasted_iota(jnp.int32, q.shape, 1)
    zero = jnp.zeros_like(q)
    return jnp.where(lane < HEAD_DIM, q, zero), jnp.where(lane >= HEAD_DIM, q, zero)


def _sub_ln(oa, g):
    ms = jnp.mean(oa * oa, axis=-1, keepdims=True)
    return oa * lax.rsqrt(ms + LN_EPS) * g * (1.0 - LAM_INIT)


def _mod_kernel(c_ref, w_ref, b_ref, o_ref):
    c = c_ref[...]
    s = (c * _sigmoid(c)).astype(BF16)
    o_ref[...] = _dot(s, w_ref[...]) + b_ref[...]


def _modulation(c_all, w_ada, b_ada):
    rows = c_all.shape[0]
    n = w_ada.shape[1]
    tn = 1536
    return pl.pallas_call(
        _mod_kernel,
        grid=(n // tn,),
        in_specs=[pl.BlockSpec((rows, D_MODEL), lambda j: (0, 0)),
                  pl.BlockSpec((D_MODEL, tn), lambda j: (0, j)),
                  pl.BlockSpec((1, tn), lambda j: (0, j))],
        out_specs=pl.BlockSpec((rows, tn), lambda j: (0, j)),
        out_shape=jax.ShapeDtypeStruct((rows, n), F32),
        name="modulation",
    )(c_all, w_ada, b_ada)


def _lam_kernel(lq1_ref, lk1_ref, lq2_ref, lk2_ref, o_ref):
    a = jnp.sum(lq1_ref[...] * lk1_ref[...], axis=-1, keepdims=True)
    b = jnp.sum(lq2_ref[...] * lk2_ref[...], axis=-1, keepdims=True)
    o_ref[...] = jnp.broadcast_to(jnp.exp(a) - jnp.exp(b) + LAM_INIT, o_ref.shape)


def _lam(lq1, lk1, lq2, lk2):
    return pl.pallas_call(
        _lam_kernel,
        out_shape=jax.ShapeDtypeStruct((1, LANES), F32),
        name="lam",
    )(lq1, lk1, lq2, lk2)


def _proj_kernel(x_ref, sc_ref, sh_ref, w_ref, wf_ref, bf_ref,
                 qa_ref, ka_ref, va_ref, qb_ref, kb_ref, vb_ref,
                 ka32_ref, va32_ref, kb32_ref, vb32_ref, lf_ref, cum_ref, carry_ref):
    i = pl.program_id(0)
    tm = x_ref.shape[0]
    h = (x_ref[...] * (1.0 + sc_ref[...]) + sh_ref[...]).astype(BF16)

    bf16_outs = (qa_ref, ka_ref, va_ref, qb_ref, kb_ref, vb_ref)
    f32_outs = (None, ka32_ref, va32_ref, None, kb32_ref, vb32_ref)
    for j in range(6):
        z = _dot(h, w_ref[:, j * A_WIDTH:(j + 1) * A_WIDTH])
        if f32_outs[j] is None:
            bf16_outs[j][...] = (z * SCALE).astype(BF16)
        else:
            f32_outs[j][...] = z
            bf16_outs[j][...] = z.astype(BF16)

    zf = _dot_nt(wf_ref[...], h) + bf_ref[...]
    logf = jnp.minimum(zf, 0.0) - jnp.log1p(jnp.exp(-jnp.abs(zf)))
    lf_ref[...] = logf

    @pl.when(i == 0)
    def _():
        carry_ref[...] = jnp.zeros_like(carry_ref)

    r = lax.broadcasted_iota(jnp.int32, (tm, tm), 0)
    c = lax.broadcasted_iota(jnp.int32, (tm, tm), 1)
    upper = jnp.where(r <= c, 1.0, 0.0).astype(F32)
    cum = _dot_f32(logf, upper) + carry_ref[:, 0:1]
    cum_ref[...] = cum
    carry_ref[...] = jnp.broadcast_to(cum[:, tm - 1:tm], carry_ref.shape)


def _projection(x, sc, sh, w_qkv, wf_t, b_forget, *, tm):
    rows = x.shape[0]
    per_row = sc.shape[0] != 1
    mod_spec = (pl.BlockSpec((tm, D_MODEL), lambda i: (i, 0)) if per_row
                else pl.BlockSpec((1, D_MODEL), lambda i: (0, 0)))
    row_spec = pl.BlockSpec((tm, A_WIDTH), lambda i: (i, 0))
    t_spec = pl.BlockSpec((N_FOX_HEADS, tm), lambda i: (0, i))
    bf = jax.ShapeDtypeStruct((rows, A_WIDTH), BF16)
    f32 = jax.ShapeDtypeStruct((rows, A_WIDTH), F32)
    tr = jax.ShapeDtypeStruct((N_FOX_HEADS, rows), F32)
    return pl.pallas_call(
        _proj_kernel,
        grid=(rows // tm,),
        in_specs=[pl.BlockSpec((tm, D_MODEL), lambda i: (i, 0)), mod_spec, mod_spec,
                  pl.BlockSpec((D_MODEL, QKV_COLS), lambda i: (0, 0)),
                  pl.BlockSpec((N_FOX_HEADS, D_MODEL), lambda i: (0, 0)),
                  pl.BlockSpec((N_FOX_HEADS, 1), lambda i: (0, 0))],
        out_specs=[row_spec] * 10 + [t_spec, t_spec],
        out_shape=[bf] * 6 + [f32] * 4 + [tr, tr],
        scratch_shapes=[pltpu.VMEM((N_FOX_HEADS, LANES), F32)],
        compiler_params=pltpu.CompilerParams(dimension_semantics=("arbitrary",),
                                             vmem_limit_bytes=48 * 1024 * 1024),
        name="projection",
    )(x, sc, sh, w_qkv, wf_t, b_forget)


def _diff_attn_kernel(rb_ref, lam_ref, q_ref, k_ref, v_ref, g_ref, o_ref, bias_ref):
    h = pl.program_id(0)
    i = pl.program_id(1)
    tq = q_ref.shape[0]
    tk = tq

    @pl.when(i == 0)
    def _():
        qi = lax.broadcasted_iota(jnp.int32, (tq, tk), 0)
        ki = lax.broadcasted_iota(jnp.int32, (tq, tk), 1)
        far = rb_ref[FAR_BUCKET, h]
        for t, off in ((0, 0), (1, -tk)):
            val = _t5_bias(ki - qi + off, rb_ref, h) - far
            visible = ((ki + off + tk) >> CHUNK_SHIFT) <= ((qi + tk) >> CHUNK_SHIFT)
            bias_ref[t] = jnp.where(visible, val, NEG)
        bias_ref[2] = jnp.full((tq, tk), NEG, F32)

    q1, q2 = _split_halves(q_ref[...])

    def chunk(c, state, bias):
        start = pl.multiple_of(c * tk, tk)
        k = k_ref[pl.ds(start, tk), :]
        v = v_ref[pl.ds(start, tk), :]
        new = []
        for qq, (m, l, acc) in zip((q1, q2), state):
            s = _dot_nt(qq, k)
            if bias is not None:
                s = s + bias
            new.append(_online_softmax_step(s, v, m, l, acc))
        return tuple(new)

    init = (jnp.full((tq, 1), NEG, F32), jnp.zeros((tq, 1), F32), jnp.zeros((tq, LANES), F32))
    state = chunk(i, (init, init), bias_ref[0])
    state = chunk(jnp.maximum(i - 1, 0), state, bias_ref[jnp.where(i == 0, 2, 1)])
    state = lax.fori_loop(0, jnp.maximum(i - 1, 0), lambda c, st: chunk(c, st, None), state)

    (_, l1, acc1), (_, l2, acc2) = state
    oa = acc1 / l1 - lam_ref[0] * (acc2 / l2)
    o_ref[...] = _sub_ln(oa, g_ref[...]).astype(o_ref.dtype)


def _diff_attention(rel_bias, lam, q, k, v, subln_g, *, tq):
    seq = q.shape[0]
    assert tq >= 128 and tq % (1 << CHUNK_SHIFT) == 0 and seq % tq == 0
    return pl.pallas_call(
        _diff_attn_kernel,
        grid=(N_DIFF_HEADS, seq // tq),
        in_specs=[pl.BlockSpec(memory_space=pltpu.SMEM),
                  pl.BlockSpec(memory_space=pltpu.SMEM),
                  pl.BlockSpec((tq, LANES), lambda h, i: (i, h)),
                  pl.BlockSpec((seq, LANES), lambda h, i: (0, h)),
                  pl.BlockSpec((seq, LANES), lambda h, i: (0, h)),
                  pl.BlockSpec((1, LANES), lambda h, i: (0, 0))],
        out_specs=pl.BlockSpec((tq, LANES), lambda h, i: (i, h)),
        out_shape=jax.ShapeDtypeStruct((seq, A_WIDTH), BF16),
        scratch_shapes=[pltpu.VMEM((3, tq, tq), F32)],
        compiler_params=pltpu.CompilerParams(dimension_semantics=("arbitrary", "arbitrary"),
                                             vmem_limit_bytes=48 * 1024 * 1024),
        name="diff_attention",
    )(rel_bias, lam, q, k, v, subln_g)


def _fox_attn_kernel(q_ref, k_ref, v_ref, f_ref, o_ref):
    i = pl.program_id(1)
    tq = q_ref.shape[0]
    tk = tq
    qa, qb = _split_halves(q_ref[...])
    diag = pl.multiple_of(i * tk, tk)
    f0 = f_ref[0, :, pl.ds(diag, tk)][:, 0:1]

    def chunk(c, state, causal):
        start = pl.multiple_of(c * tk, tk)
        k = k_ref[pl.ds(start, tk), :]
        v = v_ref[pl.ds(start, tk), :]
        dec = f0 - f_ref[0, :, pl.ds(start, tk)]
        new = []
        for j, (qq, (m, l, acc)) in enumerate(zip((qa, qb), state)):
            s = _dot_nt(qq, k) + dec[j:j + 1, :]
            if causal:
                qi = lax.broadcasted_iota(jnp.int32, s.shape, 0)
                ki = lax.broadcasted_iota(jnp.int32, s.shape, 1)
                s = jnp.where(ki <= qi, s, NEG)
            new.append(_online_softmax_step(s, v, m, l, acc))
        return tuple(new)

    init = (jnp.full((tq, 1), NEG, F32), jnp.zeros((tq, 1), F32), jnp.zeros((tq, LANES), F32))
    state = chunk(i, (init, init), True)
    state = lax.fori_loop(0, i, lambda c, st: chunk(c, st, False), state)

    (_, la, acca), (_, lb, accb) = state
    lane = lax.broadcasted_iota(jnp.int32, acca.shape, 1)
    o_ref[...] = jnp.where(lane < HEAD_DIM, acca / la, accb / lb).astype(o_ref.dtype)


def _fox_attention(q, k, v, cum_pairs, *, tq):
    seq = q.shape[0]
    pairs = N_FOX_HEADS // 2
    return pl.pallas_call(
        _fox_attn_kernel,
        grid=(pairs, seq // tq),
        in_specs=[pl.BlockSpec((tq, LANES), lambda h, i: (i, h)),
                  pl.BlockSpec((seq, LANES), lambda h, i: (0, h)),
                  pl.BlockSpec((seq, LANES), lambda h, i: (0, h)),
                  pl.BlockSpec((1, 2, seq), lambda h, i: (h, 0, 0))],
        out_specs=pl.BlockSpec((tq, LANES), lambda h, i: (i, h)),
        out_shape=jax.ShapeDtypeStruct((seq, B_WIDTH), BF16),
        compiler_params=pltpu.CompilerParams(dimension_semantics=("arbitrary", "arbitrary"),
                                             vmem_limit_bytes=48 * 1024 * 1024),
        name="fox_attention",
    )(q, k, v, cum_pairs)


def _sample_attn_kernel(rb_ref, lam_ref, qa_ref, ka_ref, va_ref, qb_ref, kb_ref, vb_ref, lfs_ref, g_ref,
                        cdk_ref, cdv_ref, cfk_ref, cfv_ref, clf_ref, oa_ref, ob_ref,
                        m_ref, l_ref, acc_ref, carry_ref, *, past):
    c = pl.program_id(1)
    nc = pl.num_programs(1)
    t_new = qa_ref.shape[0]
    tk = cdk_ref.shape[1]
    rows = 2 * t_new
    n_groups = N_DIFF_HEADS + N_FOX_HEADS // 2

    def stacked(q):
        lo, hi = _split_halves(q)
        return jnp.concatenate([lo, hi], axis=0)

    def rows_of(a, b):
        return jnp.concatenate([jnp.broadcast_to(a, (t_new, a.shape[1])),
                                jnp.broadcast_to(b, (t_new, b.shape[1]))], axis=0)

    def update(g, s, v):
        m, l, acc = _online_softmax_step(s, v, m_ref[g][:, 0:1], l_ref[g][:, 0:1], acc_ref[g])
        m_ref[g] = jnp.broadcast_to(m, (rows, LANES))
        l_ref[g] = jnp.broadcast_to(l, (rows, LANES))
        acc_ref[g] = acc

    def q_pos(shape):
        return past + (lax.broadcasted_iota(jnp.int32, shape, 0) & (t_new - 1))

    @pl.when(c == 0)
    def _():
        m_ref[...] = jnp.full(m_ref.shape, NEG, F32)
        l_ref[...] = jnp.zeros(l_ref.shape, F32)
        acc_ref[...] = jnp.zeros(acc_ref.shape, F32)
        carry_ref[...] = jnp.zeros(carry_ref.shape, F32)
        shape = (rows, t_new)
        qp = q_pos(shape)
        kp = past + lax.broadcasted_iota(jnp.int32, shape, 1)
        r = lax.broadcasted_iota(jnp.int32, (t_new, t_new), 0)
        u = lax.broadcasted_iota(jnp.int32, (t_new, t_new), 1)
        cum_new = _dot_f32(lfs_ref[0], jnp.where(r <= u, 1.0, 0.0).astype(F32))
        for h in range(N_DIFF_HEADS):
            sl = slice(h * LANES, (h + 1) * LANES)
            s = _dot_nt(stacked(qa_ref[:, sl]), ka_ref[:, sl]) + _t5_bias(kp - qp, rb_ref, h)
            s = jnp.where((kp >> CHUNK_SHIFT) <= (qp >> CHUNK_SHIFT), s, NEG)
            update(h, s, va_ref[:, sl])
        for hp in range(N_FOX_HEADS // 2):
            sl = slice(hp * LANES, (hp + 1) * LANES)
            dec = rows_of(-cum_new[2 * hp:2 * hp + 1, :], -cum_new[2 * hp + 1:2 * hp + 2, :])
            s = _dot_nt(stacked(qb_ref[:, sl]), kb_ref[:, sl]) + dec
            s = jnp.where(kp <= qp, s, NEG)
            update(N_DIFF_HEADS + hp, s, vb_ref[:, sl])

    start = (nc - 1 - c) * tk
    shape = (rows, tk)
    qp = q_pos(shape)
    kp = start + lax.broadcasted_iota(jnp.int32, shape, 1)
    for h in range(N_DIFF_HEADS):
        sl = slice(h * LANES, (h + 1) * LANES)
        s = _dot_nt(stacked(qa_ref[:, sl]), cdk_ref[0, :, sl].astype(BF16)) + _t5_bias(kp - qp, rb_ref, h)
        s = jnp.where((kp >> CHUNK_SHIFT) <= (qp >> CHUNK_SHIFT), s, NEG)
        update(h, s, cdv_ref[0, :, sl].astype(BF16))
    r = lax.broadcasted_iota(jnp.int32, (tk, tk), 0)
    u = lax.broadcasted_iota(jnp.int32, (tk, tk), 1)
    clf = clf_ref[0]
    suffix = _dot_f32(clf, jnp.where(r > u, 1.0, 0.0).astype(F32)) + carry_ref[:, 0:1]
    carry_ref[...] = carry_ref[...] + jnp.sum(clf, axis=-1, keepdims=True)
    for hp in range(N_FOX_HEADS // 2):
        sl = slice(hp * LANES, (hp + 1) * LANES)
        dec = rows_of(suffix[2 * hp:2 * hp + 1, :], suffix[2 * hp + 1:2 * hp + 2, :])
        s = _dot_nt(stacked(qb_ref[:, sl]), cfk_ref[0, :, sl].astype(BF16)) + dec
        update(N_DIFF_HEADS + hp, s, cfv_ref[0, :, sl].astype(BF16))

    @pl.when(c == nc - 1)
    def _():
        for h in range(N_DIFF_HEADS):
            o = acc_ref[h] / l_ref[h]
            oa = o[:t_new] - lam_ref[0] * o[t_new:]
            oa_ref[:, h * LANES:(h + 1) * LANES] = _sub_ln(oa, g_ref[...]).astype(oa_ref.dtype)
        for hp in range(N_FOX_HEADS // 2):
            g = N_DIFF_HEADS + hp
            o = acc_ref[g] / l_ref[g]
            lane = lax.broadcasted_iota(jnp.int32, (t_new, LANES), 1)
            ob_ref[:, hp * LANES:(hp + 1) * LANES] = jnp.where(lane < HEAD_DIM, o[:t_new], o[t_new:]).astype(ob_ref.dtype)
    del n_groups


def _sample_attention(rel_bias, lam, qa, ka, va, qb, kb, vb, lf_new, subln_g,
                      cache_dk, cache_dv, cache_fk, cache_fv, cache_lf_t, *, t_new, tk):
    n_seq, past = cache_dk.shape[0], cache_dk.shape[1]
    assert t_new & (t_new - 1) == 0 and past % tk == 0
    n_groups = N_DIFF_HEADS + N_FOX_HEADS // 2
    new_spec = pl.BlockSpec((t_new, A_WIDTH), lambda b, c: (b, 0))
    cache_spec = pl.BlockSpec((1, tk, A_WIDTH), lambda b, c: (b, past // tk - 1 - c, 0))
    smem = pl.BlockSpec(memory_space=pltpu.SMEM)
    out = jax.ShapeDtypeStruct((n_seq * t_new, A_WIDTH), BF16)
    return pl.pallas_call(
        functools.partial(_sample_attn_kernel, past=past),
        grid=(n_seq, past // tk),
        in_specs=[smem, smem, new_spec, new_spec, new_spec, new_spec, new_spec, new_spec,
                  pl.BlockSpec((1, N_FOX_HEADS, t_new), lambda b, c: (b, 0, 0)),
                  pl.BlockSpec((1, LANES), lambda b, c: (0, 0)),
                  cache_spec, cache_spec, cache_spec, cache_spec,
                  pl.BlockSpec((1, N_FOX_HEADS, tk), lambda b, c: (b, 0, past // tk - 1 - c))],
        out_specs=[new_spec, new_spec],
        out_shape=[out, out],
        scratch_shapes=[pltpu.VMEM((n_groups, 2 * t_new, LANES), F32),
                        pltpu.VMEM((n_groups, 2 * t_new, LANES), F32),
                        pltpu.VMEM((n_groups, 2 * t_new, LANES), F32),
                        pltpu.VMEM((N_FOX_HEADS, LANES), F32)],
        compiler_params=pltpu.CompilerParams(dimension_semantics=("arbitrary", "arbitrary"),
                                             vmem_limit_bytes=48 * 1024 * 1024),
        name="sample_attention",
    )(rel_bias, lam, qa, ka, va, qb, kb, vb, lf_new, subln_g,
      cache_dk, cache_dv, cache_fk, cache_fv, cache_lf_t)


def _mix_kernel(x_ref, oa_ref, ob_ref, sc_ref, sh_ref, gate_ref, wg_ref, wa_ref, wb_ref, wo_ref,
                lng_ref, lnb_ref, o_ref):
    x = x_ref[...]
    h = (x * (1.0 + sc_ref[...]) + sh_ref[...]).astype(BF16)
    gates = _dot(h, wg_ref[...])
    pa = _dot(oa_ref[...], wa_ref[...])
    pb = _dot(ob_ref[...], wb_ref[...])
    y = _sigmoid(gates[:, :D_MODEL]) * pa + _sigmoid(gates[:, D_MODEL:]) * pb
    y = _dot(y.astype(BF16), wo_ref[...])
    o_ref[...] = _layer_norm(ALPHA * x + gate_ref[...] * y, lng_ref[...], lnb_ref[...])


def _mixer_out(x, oa, ob, sc, sh, gate, w_g, w_a, w_b, w_o, ln_g, ln_b, *, tm):
    rows = x.shape[0]
    per_row = sc.shape[0] != 1
    mod_spec = (pl.BlockSpec((tm, D_MODEL), lambda i: (i, 0)) if per_row
                else pl.BlockSpec((1, D_MODEL), lambda i: (0, 0)))
    vec_spec = pl.BlockSpec((1, D_MODEL), lambda i: (0, 0))
    row_spec = pl.BlockSpec((tm, D_MODEL), lambda i: (i, 0))
    half_spec = pl.BlockSpec((tm, A_WIDTH), lambda i: (i, 0))
    return pl.pallas_call(
        _mix_kernel,
        grid=(rows // tm,),
        in_specs=[row_spec, half_spec, half_spec, mod_spec, mod_spec, mod_spec,
                  pl.BlockSpec((D_MODEL, 2 * D_MODEL), lambda i: (0, 0)),
                  pl.BlockSpec((A_WIDTH, D_MODEL), lambda i: (0, 0)),
                  pl.BlockSpec((B_WIDTH, D_MODEL), lambda i: (0, 0)),
                  pl.BlockSpec((D_MODEL, D_MODEL), lambda i: (0, 0)),
                  vec_spec, vec_spec],
        out_specs=row_spec,
        out_shape=jax.ShapeDtypeStruct((rows, D_MODEL), F32),
        compiler_params=pltpu.CompilerParams(dimension_semantics=("arbitrary",),
                                             vmem_limit_bytes=48 * 1024 * 1024),
        name="mixer_out",
    )(x, oa, ob, sc, sh, gate, w_g, w_a, w_b, w_o, ln_g, ln_b)


def _ffn_kernel(x_ref, sc_ref, sh_ref, gate_ref, wg_ref, wu_ref, wo_ref, lng_ref, lnb_ref, o_ref,
                h_ref, acc_ref):
    j = pl.program_id(1)

    @pl.when(j == 0)
    def _():
        h_ref[...] = (x_ref[...] * (1.0 + sc_ref[...]) + sh_ref[...]).astype(BF16)
        acc_ref[...] = jnp.zeros_like(acc_ref)

    h = h_ref[...]
    g = _dot(h, wg_ref[...])
    u = _dot(h, wu_ref[...])
    a = (g * _sigmoid(g) * u).astype(BF16)
    acc_ref[...] += _dot(a, wo_ref[...])

    @pl.when(j == pl.num_programs(1) - 1)
    def _():
        r = ALPHA * x_ref[...] + gate_ref[...] * acc_ref[...]
        o_ref[...] = _layer_norm(r, lng_ref[...], lnb_ref[...])


def _ffn(x, sc, sh, gate, w_in, w_out, ln_g, ln_b, *, tm):
    rows = x.shape[0]
    per_row = sc.shape[0] != 1
    n_col = D_FF // FFN_COL
    mod_spec = (pl.BlockSpec((tm, D_MODEL), lambda i, j: (i, 0)) if per_row
                else pl.BlockSpec((1, D_MODEL), lambda i, j: (0, 0)))
    vec_spec = pl.BlockSpec((1, D_MODEL), lambda i, j: (0, 0))
    row_spec = pl.BlockSpec((tm, D_MODEL), lambda i, j: (i, 0))
    return pl.pallas_call(
        _ffn_kernel,
        grid=(rows // tm, n_col),
        in_specs=[row_spec, mod_spec, mod_spec, mod_spec,
                  pl.BlockSpec((D_MODEL, FFN_COL), lambda i, j: (0, j)),
                  pl.BlockSpec((D_MODEL, FFN_COL), lambda i, j: (0, n_col + j)),
                  pl.BlockSpec((FFN_COL, D_MODEL), lambda i, j: (j, 0)),
                  vec_spec, vec_spec],
        out_specs=row_spec,
        out_shape=jax.ShapeDtypeStruct((rows, D_MODEL), F32),
        scratch_shapes=[pltpu.VMEM((tm, D_MODEL), BF16), pltpu.VMEM((tm, D_MODEL), F32)],
        compiler_params=pltpu.CompilerParams(dimension_semantics=("arbitrary", "arbitrary"),
                                             vmem_limit_bytes=48 * 1024 * 1024),
        name="ffn",
    )(x, sc, sh, gate, w_in, w_in, w_out, ln_g, ln_b)


def kernel(x_prompt, x_sample, cache_diff_k, cache_diff_v, cache_fox_k, cache_fox_v, cache_fox_logf, c_prompt, c_sample, w_ada, b_ada, w_in, b_forget, lambda_q1, lambda_k1, lambda_q2, lambda_k2, subln_g, rel_bias, w_branch_a, w_branch_b, w_o, ln1_g, ln1_b, ln2_g, ln2_b, w_ffn_in, w_ffn_out):
    depth, n_seq, past = cache_diff_k.shape[:3]
    assert depth == 1 and x_prompt.shape[0] == 1
    seq = x_prompt.shape[1]
    t_new = x_sample.shape[1]
    n_prompt = c_prompt.shape[0]

    w_in0 = w_in[0]
    w_qkv = w_in0[:, :QKV_COLS].astype(BF16)
    wf_t = w_in0[:, QKV_COLS:QKV_COLS + N_FOX_HEADS].T.astype(BF16)
    w_g = w_in0[:, QKV_COLS + N_FOX_HEADS:].astype(BF16)
    w_a = w_branch_a[0].astype(BF16)
    w_b = w_branch_b[0].astype(BF16)
    w_out = w_o[0].astype(BF16)
    w_f_in = w_ffn_in[0].astype(BF16)
    w_f_out = w_ffn_out[0].astype(BF16)
    b_f = b_forget[0].reshape(N_FOX_HEADS, 1)
    g_sub = subln_g[0].reshape(1, LANES)
    row = lambda a: a[0].reshape(1, -1)

    pad = (-(n_prompt + n_seq)) % 8
    c_all = jnp.concatenate([c_prompt, c_sample, jnp.zeros((pad, D_MODEL), F32)], axis=0)
    mod = _modulation(c_all, w_ada[0].astype(BF16), b_ada[0].reshape(1, -1))
    mod_p = [mod[0:1, j * D_MODEL:(j + 1) * D_MODEL] for j in range(6)]
    mod_s_rows = jnp.repeat(mod[n_prompt:n_prompt + n_seq], t_new, axis=0)
    mod_s = [mod_s_rows[:, j * D_MODEL:(j + 1) * D_MODEL] for j in range(6)]
    lam = _lam(row(lambda_q1), row(lambda_k1), row(lambda_q2), row(lambda_k2))[0, 0:1]

    def layer(x, m, attend, tm_proj, tm_mix, tm_ffn):
        shift1, scale1, gate1, shift2, scale2, gate2 = m
        (qa, ka, va, qb, kb, vb, ka32, va32, kb32, vb32, lf_t, cum_t) = _projection(
            x, scale1, shift1, w_qkv, wf_t, b_f, tm=tm_proj)
        oa, ob = attend(qa, ka, va, qb, kb, vb, lf_t, cum_t)
        x1 = _mixer_out(x, oa, ob, scale1, shift1, gate1, w_g, w_a, w_b, w_out, row(ln1_g), row(ln1_b), tm=tm_mix)
        x2 = _ffn(x1, scale2, shift2, gate2, w_f_in, w_f_out, row(ln2_g), row(ln2_b), tm=tm_ffn)
        return x2, ka32, va32, kb32, vb32, lf_t

    def attend_prompt(qa, ka, va, qb, kb, vb, lf_t, cum_t):
        oa = _diff_attention(rel_bias, lam, qa, ka, va, g_sub, tq=256)
        ob = _fox_attention(qb, kb, vb, cum_t.reshape(N_FOX_HEADS // 2, 2, seq), tq=256)
        return oa, ob

    def attend_sample(qa, ka, va, qb, kb, vb, lf_t, cum_t):
        lf_new = lf_t.reshape(N_FOX_HEADS, n_seq, t_new).transpose(1, 0, 2)
        cache_lf_t = cache_fox_logf[0].transpose(0, 2, 1)
        return _sample_attention(
            rel_bias, lam, qa, ka, va, qb, kb, vb, lf_new, g_sub,
            cache_diff_k[0].reshape(n_seq, past, A_WIDTH), cache_diff_v[0].reshape(n_seq, past, A_WIDTH),
            cache_fox_k[0].reshape(n_seq, past, B_WIDTH), cache_fox_v[0].reshape(n_seq, past, B_WIDTH),
            cache_lf_t, t_new=t_new, tk=512)

    yp, dk_p, dv_p, fk_p, fv_p, lf_p = layer(x_prompt[0], mod_p, attend_prompt, 256, 256, 512)
    ys, dk_s, dv_s, fk_s, fv_s, lf_s = layer(x_sample.reshape(n_seq * t_new, D_MODEL), mod_s, attend_sample,
                                             n_seq * t_new, n_seq * t_new, n_seq * t_new)

    return (yp.reshape(1, seq, D_MODEL), ys.reshape(n_seq, t_new, D_MODEL),
            dk_p.reshape(1, 1, seq, N_DIFF_HEADS, 2 * HEAD_DIM), dv_p.reshape(1, 1, seq, N_DIFF_HEADS, 2 * HEAD_DIM),
            fk_p.reshape(1, 1, seq, N_FOX_HEADS, HEAD_DIM), fv_p.reshape(1, 1, seq, N_FOX_HEADS, HEAD_DIM),
            lf_p.T.reshape(1, 1, seq, N_FOX_HEADS),
            dk_s.reshape(1, n_seq, t_new, N_DIFF_HEADS, 2 * HEAD_DIM), dv_s.reshape(1, n_seq, t_new, N_DIFF_HEADS, 2 * HEAD_DIM),
            fk_s.reshape(1, n_seq, t_new, N_FOX_HEADS, HEAD_DIM), fv_s.reshape(1, n_seq, t_new, N_FOX_HEADS, HEAD_DIM),
            lf_s.T.reshape(1, n_seq, t_new, N_FOX_HEADS))
```
